```python
import jax, jax.numpy as jnp
from jax import lax
import numpy as np

D_MODEL = 1024
BATCH = 2
SEQ = 8192
DEPTH = 1
DEC_BATCH = 128
DEC_SEQ = 4
PAST_LEN = 2048
PAGE_SIZE = 128

N_META = 16
A_HEADS = 8
A_HEAD_DIM = 64
A_WIDTH = A_HEADS * A_HEAD_DIM
B_HEADS = 4
B_KEY_DIM = 128
B_VAL_DIM = 128
B_KEY_WIDTH = B_HEADS * B_KEY_DIM
B_WIDTH = B_HEADS * B_VAL_DIM
MIX_WIDTH = A_WIDTH + B_WIDTH
IN_SIZES = (A_WIDTH, A_WIDTH, A_WIDTH, A_HEADS, A_WIDTH, B_KEY_WIDTH, B_KEY_WIDTH, B_WIDTH, B_WIDTH)
IN_WIDTH = sum(IN_SIZES)
BLOCK = 128
CHUNK = 128
RMS_EPS = 1e-6

kernel_name = "fox_hgrn2_parallel_heads_step"


def rmsnorm(x, g):
    xf = x.astype(jnp.float32)
    y = xf * lax.rsqrt(jnp.mean(xf * xf, axis=-1, keepdims=True) + RMS_EPS)
    return (y * g.astype(jnp.float32)).astype(x.dtype)


def branch_inputs(hn, w_in, b_forget, lb):
    bsz, t = hn.shape[:2]
    u = hn @ w_in
    offs = [int(o) for o in np.cumsum(IN_SIZES)[:-1]]
    qa, ka, va, fa, za, qb, fb, vb, zb = jnp.split(u, offs, axis=-1)
    qa = qa.reshape(bsz, t, A_HEADS, A_HEAD_DIM)
    ka = ka.reshape(bsz, t, A_HEADS, A_HEAD_DIM)
    va = va.reshape(bsz, t, A_HEADS, A_HEAD_DIM)
    logf_a = jax.nn.log_sigmoid(fa.astype(jnp.float32) + b_forget.astype(jnp.float32))
    lbh = lb.reshape(B_HEADS, B_KEY_DIM)
    g = lbh + (1.0 - lbh) * jax.nn.sigmoid(fb.astype(jnp.float32).reshape(bsz, t, B_HEADS, B_KEY_DIM))
    kb = 1.0 - g
    logf_b = jnp.log(g)
    qb = jax.nn.silu(qb.astype(jnp.float32)).reshape(bsz, t, B_HEADS, B_KEY_DIM)
    vb = vb.reshape(bsz, t, B_HEADS, B_VAL_DIM)
    return qa, ka, va, logf_a, za, qb, kb, vb, logf_b, zb


def fox_attend(q, cq, k, ck, v, mask):
    scale = A_HEAD_DIM ** -0.5
    s = jnp.einsum('bqhd,bkhd->bhqk', q, k).astype(jnp.float32) * scale
    s = s + jnp.transpose(cq, (0, 2, 1))[:, :, :, None] - jnp.transpose(ck, (0, 2, 1))[:, :, None, :]
    s = jnp.where(mask, s, -jnp.inf)
    p = jax.nn.softmax(s, axis=-1)
    return jnp.einsum('bhqk,bkhd->bqhd', p.astype(v.dtype), v)


def fox_prompt(q, k, v, logf):
    bsz, L = q.shape[:2]
    c = jnp.cumsum(logf, axis=1)
    meta_mask = jnp.tril(jnp.ones((N_META, N_META), dtype=bool))
    o_meta = fox_attend(q[:, :N_META], c[:, :N_META], k[:, :N_META], c[:, :N_META], v[:, :N_META], meta_mask)
    n_blk = (L - N_META) // BLOCK
    k_pos = jnp.arange(L)

    def one_block(b):
        start = N_META + b * BLOCK
        qblk = lax.dynamic_slice_in_dim(q, start, BLOCK, axis=1)
        cblk = lax.dynamic_slice_in_dim(c, start, BLOCK, axis=1)
        q_pos = start + jnp.arange(BLOCK)
        mask = k_pos[None, :] <= q_pos[:, None]
        return fox_attend(qblk, cblk, k, c, v, mask)

    o_blocks = lax.map(one_block, jnp.arange(n_blk))
    o_real = jnp.swapaxes(o_blocks, 0, 1).reshape(bsz, n_blk * BLOCK, A_HEADS, A_HEAD_DIM)
    return jnp.concatenate([o_meta, o_real], axis=1)


def fox_sample(q, k, v, logf, past_k, past_v, past_logf):
    c_past = jnp.cumsum(past_logf.astype(jnp.float32), axis=1)
    c_new = c_past[:, -1:] + jnp.cumsum(logf, axis=1)
    keys = jnp.concatenate([past_k.astype(k.dtype), k], axis=1)
    vals = jnp.concatenate([past_v.astype(v.dtype), v], axis=1)
    c_all = jnp.concatenate([c_past, c_new], axis=1)
    P, T = past_k.shape[1], q.shape[1]
    mask = jnp.arange(P + T)[None, :] <= (P + jnp.arange(T))[:, None]
    return fox_attend(q, c_new, keys, c_all, vals, mask)


def hgrn_chunk(S0, q, k, v, logf):
    S0 = S0.astype(jnp.float32)
    q = q.astype(jnp.float32)
    k = k.astype(jnp.float32)
    v = v.astype(jnp.float32)
    b = jnp.cumsum(logf.astype(jnp.float32), axis=1)
    o_inter = jnp.einsum('bthk,bhkv->bthv', q * jnp.exp(b), S0)
    C = q.shape[1]
    causal = jnp.tril(jnp.ones((C, C), dtype=bool))
    diff = b[:, :, None] - b[:, None, :]
    decay = jnp.exp(jnp.where(causal[None, :, :, None, None], diff, -jnp.inf))
    a = jnp.einsum('bthk,bshk,btshk->bhts', q, k, decay)
    o_intra = jnp.einsum('bhts,bshv->bthv', a, v)
    b_last = b[:, -1]
    S_new = jnp.exp(b_last)[..., None] * S0 + jnp.einsum('bshk,bshv->bhkv', k * jnp.exp(b_last[:, None] - b), v)
    return o_inter + o_intra, S_new


def hgrn_prompt(q, k, v, logf):
    bsz, L = q.shape[:2]
    S0 = jnp.zeros((bsz, B_HEADS, B_KEY_DIM, B_VAL_DIM), jnp.float32)
    o_meta, S = hgrn_chunk(S0, q[:, :N_META], k[:, :N_META], v[:, :N_META], logf[:, :N_META])
    n_c = (L - N_META) // CHUNK

    def to_chunks(a):
        return jnp.swapaxes(a[:, N_META:].reshape(bsz, n_c, CHUNK, *a.shape[2:]), 0, 1)

    def step(S, xs):
        o, S = hgrn_chunk(S, *xs)
        return S, o

    S, o_c = lax.scan(step, S, (to_chunks(q), to_chunks(k), to_chunks(v), to_chunks(logf)))
    o_real = jnp.swapaxes(o_c, 0, 1).reshape(bsz, n_c * CHUNK, B_HEADS, B_VAL_DIM)
    return jnp.concatenate([o_meta, o_real], axis=1), S


def merge_heads(o_a, za, o_b, zb, out_norm, w_out):
    bsz, t = o_a.shape[:2]
    ya = o_a.reshape(bsz, t, A_WIDTH) * jax.nn.silu(za)
    yb = rmsnorm(o_b, out_norm).reshape(bsz, t, B_WIDTH) * jax.nn.silu(zb)
    return jnp.concatenate([ya.astype(yb.dtype), yb], axis=-1) @ w_out


def setup_inputs(seed: int = 0) -> dict:
    key = jax.random.key(seed)
    ks = jax.random.split(key, 16)
    n_pages = PAST_LEN // PAGE_SIZE
    n_pool = (DEC_BATCH * n_pages * 5) // 4
    f32 = jnp.float32
    x_prompt = jax.random.normal(ks[0], (BATCH, SEQ, D_MODEL), f32)
    x_sample = jax.random.normal(ks[1], (DEC_BATCH, DEC_SEQ, D_MODEL), f32)
    cache_k = jax.random.normal(ks[2], (DEPTH, n_pool, PAGE_SIZE, A_HEADS, A_HEAD_DIM), f32)
    cache_v = jax.random.normal(ks[3], (DEPTH, n_pool, PAGE_SIZE, A_HEADS, A_HEAD_DIM), f32)
    cache_logf = jax.nn.log_sigmoid(3.0 + jax.random.normal(ks[4], (DEPTH, n_pool, PAGE_SIZE, A_HEADS), f32))
    state_hgrn = jax.random.normal(ks[5], (DEPTH, DEC_BATCH, B_HEADS, B_KEY_DIM, B_VAL_DIM), f32)
    perm = jax.random.permutation(ks[6], n_pool)
    page_table = perm[: DEC_BATCH * n_pages].reshape(DEC_BATCH, n_pages).astype(jnp.int32)
    meta_tokens = jax.random.normal(ks[7], (N_META, D_MODEL), f32)
    w_in = jax.random.normal(ks[8], (DEPTH, D_MODEL, IN_WIDTH), f32) * D_MODEL ** -0.5
    b_forget = 3.0 + 0.1 * jax.random.normal(ks[9], (DEPTH, A_HEADS), f32)
    hgrn_lower_bound = 0.1 * jax.random.normal(ks[10], (DEPTH + 1, B_KEY_WIDTH), f32)
    hgrn_out_norm = 1.0 + 0.01 * jax.random.normal(ks[11], (DEPTH, B_VAL_DIM), f32)
    pre_norm = 1.0 + 0.01 * jax.random.normal(ks[12], (DEPTH, D_MODEL), f32)
    post_norm = 1.0 + 0.01 * jax.random.normal(ks[13], (DEPTH, D_MODEL), f32)
    w_out = jax.random.normal(ks[14], (DEPTH, MIX_WIDTH, D_MODEL), f32) * MIX_WIDTH ** -0.5
    return {"x_prompt": x_prompt, "x_sample": x_sample, "cache_k": cache_k, "cache_v": cache_v,
            "cache_logf": cache_logf, "state_hgrn": state_hgrn, "page_table": page_table,
            "meta_tokens": meta_tokens, "w_in": w_in, "b_forget": b_forget,
            "hgrn_lower_bound": hgrn_lower_bound, "hgrn_out_norm": hgrn_out_norm,
            "pre_norm": pre_norm, "post_norm": post_norm, "w_out": w_out}


def reference(x_prompt, x_sample, cache_k, cache_v, cache_logf, state_hgrn, page_table,
              meta_tokens, w_in, b_forget, hgrn_lower_bound, hgrn_out_norm, pre_norm, post_norm, w_out):
    bp = x_prompt.shape[0]
    bs = x_sample.shape[0]
    n_pages = page_table.shape[1]
    page = cache_k.shape[2]
    meta = jnp.broadcast_to(meta_tokens[None].astype(x_prompt.dtype), (bp, N_META, D_MODEL))
    hp = jnp.concatenate([meta, x_prompt], axis=1)
    hs = x_sample
    lb_all = jnp.cumsum(jax.nn.softmax(hgrn_lower_bound.astype(jnp.float32), axis=0), axis=0)
    pk, pv, plf, pS, sk, sv, slf, sS = [], [], [], [], [], [], [], []
    for l in range(DEPTH):
        lb = lb_all[l]
        hn = rmsnorm(hp, pre_norm[l])
        qa, ka, va, lfa, za, qb, kb, vb, lfb, zb = branch_inputs(hn, w_in[l], b_forget[l], lb)
        oa = fox_prompt(qa, ka, va, lfa)
        ob, S_p = hgrn_prompt(qb, kb, vb, lfb)
        hp = hp + rmsnorm(merge_heads(oa, za, ob, zb, hgrn_out_norm[l], w_out[l]), post_norm[l])
        pk.append(ka); pv.append(va); plf.append(lfa); pS.append(S_p)
        hn = rmsnorm(hs, pre_norm[l])
        qa, ka, va, lfa, za, qb, kb, vb, lfb, zb = branch_inputs(hn, w_in[l], b_forget[l], lb)
        past_k = cache_k[l][page_table].reshape(bs, n_pages * page, A_HEADS, A_HEAD_DIM)
        past_v = cache_v[l][page_table].reshape(bs, n_pages * page, A_HEADS, A_HEAD_DIM)
        past_lf = cache_logf[l][page_table].reshape(bs, n_pages * page, A_HEADS)
        oa = fox_sample(qa, ka, va, lfa, past_k, past_v, past_lf)
        ob, S_s = hgrn_chunk(state_hgrn[l], qb, kb, vb, lfb)
        hs = hs + rmsnorm(merge_heads(oa, za, ob, zb, hgrn_out_norm[l], w_out[l]), post_norm[l])
        sk.append(ka); sv.append(va); slf.append(lfa); sS.append(S_s)
    y_prompt = hp[:, N_META:]
    return (y_prompt, hs, jnp.stack(pk), jnp.stack(pv), jnp.stack(plf), jnp.stack(pS),
            jnp.stack(sk), jnp.stack(sv), jnp.stack(slf), jnp.stack(sS))
```

```python
import functools

import numpy as np
import jax
import jax.numpy as jnp
from jax import lax
from jax.experimental import pallas as pl
from jax.experimental.pallas import tpu as pltpu

F32 = jnp.float32
BF16 = jnp.bfloat16

N_META = 16
A_HEADS = 8
A_HEAD_DIM = 64
A_WIDTH = A_HEADS * A_HEAD_DIM
B_HEADS = 4
B_DIM = 128
B_WIDTH = B_HEADS * B_DIM
RMS_EPS = 1e-6
LANES = 128
SUBLANES = 8
NEG = -1e30
ROW_TILE = 256
FOX_TQ = 768
FOX_TK = 256
HGRN_ROWS = 768
CHUNK = 128
BIAS_GROUPS = 6
MASK_LANE = BIAS_GROUPS * A_HEADS
PAGE_BLOCK = 256
DEC_TOK = 8
HGRN_DEC_BATCH = 4
VMEM_LIMIT = 56 * 1024 * 1024

C_QA, C_KA, C_VA, C_ZA, C_QB, C_FB, C_VB, C_ZB, C_FA = (0, 512, 1024, 1536, 2048, 2560, 3072, 3584, 4096)
PACKED_WIDTH = C_FA + LANES


def _params(sem):
    return pltpu.CompilerParams(dimension_semantics=sem, vmem_limit_bytes=VMEM_LIMIT)


def _split3(x):
    hi = x.astype(BF16)
    r = x - hi.astype(F32)
    mid = r.astype(BF16)
    lo = (r - mid.astype(F32)).astype(BF16)
    return hi, mid, lo


def _dot(a, b):
    return jnp.dot(a, b, preferred_element_type=F32)


def _dot_nt(a, b):
    return lax.dot_general(a, b, (((1,), (1,)), ((), ())), preferred_element_type=F32)


def _tri_dot(tri, x):
    hi, mid, lo = _split3(x)
    return _dot(tri, hi) + _dot(tri, mid) + _dot(tri, lo)


def _silu(x):
    return x * (1.0 / (1.0 + jnp.exp(-x)))


def _log_sigmoid(x):
    return jnp.minimum(x, 0.0) - jnp.log1p(jnp.exp(-jnp.abs(x)))


def _in_kernel(prompt, pad, x_ref, meta_ref, w_ref, g_ref, bf_ref, lb_ref, tria_ref, trib_ref,
               q_ref, k32_ref, v32_ref, kbf_ref, vbf_ref, qbias_ref, kbias_ref, lfa_ref, ca_ref,
               sza_ref, qb_ref, kb_ref, vb_ref, bcum_ref, szb_ref, carry_ref):
    j = pl.program_id(1)
    tm = x_ref.shape[1]
    x = x_ref[0]
    if prompt:
        x = jnp.where(j == 0, meta_ref[...], x)
        rows = lax.broadcasted_iota(jnp.int32, (tm, 1), 0)
        valid = jnp.logical_or(j > 0, rows >= pad)

        @pl.when(j == 0)
        def _():
            carry_ref[...] = jnp.zeros_like(carry_ref)

    ms = jnp.mean(x * x, axis=-1, keepdims=True)
    hn = (x * lax.rsqrt(ms + RMS_EPS) * g_ref[...]).astype(BF16)

    def proj(c0, n):
        return _dot(hn, w_ref[:, c0:c0 + n])

    lane = lax.broadcasted_iota(jnp.int32, (1, LANES), 1)
    grp = lane // A_HEADS

    lf = _log_sigmoid(proj(C_FA, LANES) + bf_ref[...])
    keep = lane < MASK_LANE
    if prompt:
        keep = jnp.logical_and(keep, valid)
    lf = jnp.where(keep, lf, 0.0)
    lfa_ref[0] = lf[:, :A_HEADS]
    c = _tri_dot(tria_ref[...], lf)
    if prompt:
        c = c + carry_ref[0:1, :]
        carry_ref[...] = jnp.broadcast_to(c[tm - 1:tm, :], carry_ref.shape)
    ca_ref[0] = c
    if prompt:
        ch, cm, cl = (p.astype(F32) for p in _split3(c))
        one = jnp.float32(1.0)
        qbias = jnp.where(grp == 0, ch, jnp.where(grp == 1, cm, jnp.where(grp == 2, cl,
                          jnp.where(lane <= MASK_LANE, one, 0.0))))
        padv = jnp.where(valid, 0.0, NEG)
        kbias = jnp.where(grp < 3, one, jnp.where(grp == 3, -ch, jnp.where(grp == 4, -cm,
                          jnp.where(grp == 5, -cl, jnp.where(lane == MASK_LANE, padv, 0.0)))))
        qbias_ref[0] = qbias.astype(BF16)
        kbias_ref[0] = kbias.astype(BF16)
    else:
        qbias_ref[0] = jnp.zeros(qbias_ref.shape[1:], BF16)
        kbias_ref[0] = jnp.zeros(kbias_ref.shape[1:], BF16)

    q_ref[0] = (proj(C_QA, A_WIDTH) * (A_HEAD_DIM ** -0.5)).astype(BF16)
    k = proj(C_KA, A_WIDTH)
    k32_ref[0] = k
    kbf_ref[0] = k.astype(BF16)
    v = proj(C_VA, A_WIDTH)
    v32_ref[0] = v
    vbf_ref[0] = v.astype(BF16)
    sza_ref[0] = _silu(proj(C_ZA, A_WIDTH)).astype(BF16)

    qb_ref[0] = _silu(proj(C_QB, B_WIDTH)).astype(BF16)
    lbr = lb_ref[...]
    e = jnp.exp(lbr - jnp.max(lbr, axis=0, keepdims=True))
    lb = e[0:1, :] / jnp.sum(e, axis=0, keepdims=True)
    fb = proj(C_FB, B_WIDTH)
    gate = lb + (1.0 - lb) * (1.0 / (1.0 + jnp.exp(-fb)))
    kb = 1.0 - gate
    lfb = jnp.log(gate)
    if prompt:
        kb = jnp.where(valid, kb, 0.0)
        lfb = jnp.where(valid, lfb, 0.0)
    kb_ref[0] = kb.astype(BF16)
    bcum_ref[0] = _tri_dot(trib_ref[...], lfb)
    vb_ref[0] = proj(C_VB, B_WIDTH).astype(BF16)
    szb_ref[0] = _silu(proj(C_ZB, B_WIDTH)).astype(BF16)


def _block_tril(n, blk):
    r = np.arange(n)
    m = (r[:, None] >= r[None, :]) & ((r[:, None] // blk) == (r[None, :] // blk))
    return jnp.asarray(m, dtype=BF16)


def _in_projection(x, meta_blk, wp, g, bf_rep, lbraw, prompt, tri_a_blk, tri_b_blk):
    bsz, rows, d = x.shape
    tm = ROW_TILE
    nb = rows // tm + (1 if prompt else 0)
    rp = nb * tm
    pad = tm - N_META
    tri_a = _block_tril(tm, tri_a_blk)
    tri_b = _block_tril(tm, tri_b_blk)
    if prompt:
        x_map = lambda b, j: (b, jnp.maximum(j - 1, 0), 0)
    else:
        x_map = lambda b, j: (b, j, 0)
    const2 = lambda b, j: (0, 0)
    row_map = lambda b, j: (b, j, 0)

    def out(n, dt):
        return jax.ShapeDtypeStruct((bsz, rp, n), dt), pl.BlockSpec((1, tm, n), row_map)

    outs = [out(A_WIDTH, BF16), out(A_WIDTH, F32), out(A_WIDTH, F32), out(A_WIDTH, BF16), out(A_WIDTH, BF16),
            out(LANES, BF16), out(LANES, BF16), out(A_HEADS, F32), out(LANES, F32),
            out(A_WIDTH, BF16), out(B_WIDTH, BF16), out(B_WIDTH, BF16), out(B_WIDTH, BF16),
            out(B_WIDTH, F32), out(B_WIDTH, BF16)]
    return pl.pallas_call(
        functools.partial(_in_kernel, prompt, pad),
        grid=(bsz, nb),
        in_specs=[pl.BlockSpec((1, tm, d), x_map),
                  pl.BlockSpec((tm, d), const2),
                  pl.BlockSpec(wp.shape, const2),
                  pl.BlockSpec(g.shape, const2),
                  pl.BlockSpec(bf_rep.shape, const2),
                  pl.BlockSpec(lbraw.shape, const2),
                  pl.BlockSpec((tm, tm), const2),
                  pl.BlockSpec((tm, tm), const2)],
        out_specs=[o[1] for o in outs],
        out_shape=[o[0] for o in outs],
        scratch_shapes=[pltpu.VMEM((SUBLANES, LANES), F32)],
        compiler_params=_params(("arbitrary", "arbitrary")),
        name="in_projection_prompt" if prompt else "in_projection_sample",
    )(x, meta_blk, wp, g, bf_rep, lbraw, tri_a, tri_b)


def _fox_kernel(tq, tk, q_ref, qb_ref, k_ref, kb_ref, v_ref, o_ref, m_scr, l_scr, acc_scr):
    pair = pl.program_id(1)
    qi = pl.program_id(2)
    lane = lax.broadcasted_iota(jnp.int32, (1, LANES), 1)
    first = lane < A_HEAD_DIM
    q = q_ref[0]
    qb = qb_ref[0]
    zq = jnp.zeros_like(q)
    hl = lane % A_HEADS
    in_groups = lane < MASK_LANE
    is_mask = lane == MASK_LANE
    qas = []
    for hh in range(2):
        qh = jnp.where(first, q, zq) if hh == 0 else jnp.where(first, zq, q)
        sel = jnp.logical_or(jnp.logical_and(hl == 2 * pair + hh, in_groups), is_mask)
        qas.append(jnp.concatenate([qh, jnp.where(sel, qb, zq)], axis=1))

    m_scr[...] = jnp.full(m_scr.shape, NEG, F32)
    l_scr[...] = jnp.zeros(l_scr.shape, F32)
    acc_scr[...] = jnp.zeros(acc_scr.shape, F32)
    reps = tk // LANES

    def step(j, masked):
        ks = pl.multiple_of(j * tk, tk)
        ka = jnp.concatenate([k_ref[0, pl.ds(ks, tk), :], kb_ref[0, pl.ds(ks, tk), :]], axis=1)
        vblk = v_ref[0, pl.ds(ks, tk), :]
        zv = jnp.zeros_like(vblk)
        if masked:
            row = qi * tq + lax.broadcasted_iota(jnp.int32, (tq, tk), 0)
            col = ks + lax.broadcasted_iota(jnp.int32, (tq, tk), 1)
            causal = col <= row
        pv = None
        alphas = []
        for hh in range(2):
            s = _dot_nt(qas[hh], ka)
            if masked:
                s = jnp.where(causal, s, NEG)
            m_prev = m_scr[hh]
            m_next = jnp.maximum(m_prev, jnp.max(s, axis=1, keepdims=True))
            alpha = jnp.exp(m_prev - m_next)
            p = jnp.exp(s - pltpu.repeat(m_next, reps, axis=1))
            l_scr[hh] = alpha * l_scr[hh] + jnp.sum(p, axis=1, keepdims=True)
            m_scr[hh] = m_next
            vh = jnp.where(first, vblk, zv) if hh == 0 else jnp.where(first, zv, vblk)
            d = _dot(p.astype(BF16), vh)
            pv = d if pv is None else pv + d
            alphas.append(alpha)
        acc_scr[...] = acc_scr[...] * jnp.where(first, alphas[0], alphas[1]) + pv

    ratio = tq // tk
    n_full = qi * ratio

    def body(j, carry):
        step(j, False)
        return carry

    lax.fori_loop(0, n_full, body, 0)
    for dd in range(ratio):
        step(n_full + dd, True)
    o_ref[0] = acc_scr[...] / jnp.where(first, l_scr[0], l_scr[1])


def _fox_prompt(q, qbias, kbf, kbias, vbf):
    bsz, lp, _ = q.shape
    tq, tk = FOX_TQ, FOX_TK
    nq = lp // tq
    pairs = A_HEADS // 2
    return pl.pallas_call(
        functools.partial(_fox_kernel, tq, tk),
        grid=(bsz, pairs, nq),
        in_specs=[pl.BlockSpec((1, tq, LANES), lambda b, p, i: (b, i, p)),
                  pl.BlockSpec((1, tq, LANES), lambda b, p, i: (b, i, 0)),
                  pl.BlockSpec((1, lp, LANES), lambda b, p, i: (b, 0, p)),
                  pl.BlockSpec((1, lp, LANES), lambda b, p, i: (b, 0, 0)),
                  pl.BlockSpec((1, lp, LANES), lambda b, p, i: (b, 0, p))],
        out_specs=pl.BlockSpec((1, tq, LANES), lambda b, p, i: (b, i, p)),
        out_shape=jax.ShapeDtypeStruct((bsz, lp, A_WIDTH), F32),
        scratch_shapes=[pltpu.VMEM((2, tq, LANES), F32), pltpu.VMEM((2, tq, LANES), F32),
                        pltpu.VMEM((tq, LANES), F32)],
        compiler_params=_params(("arbitrary", "arbitrary", "arbitrary")),
        name="fox_prompt",
    )(q, qbias, kbf, kbias, vbf)


def _level_table(c):
    r = np.arange(c)
    x = r[:, None] ^ r[None, :]
    lv = np.floor(np.log2(np.maximum(x, 1))).astype(np.int32)
    lv = np.where(x == 0, -1, lv)
    lv = np.where(r[:, None] < r[None, :], -2, lv)
    return jnp.asarray(lv, dtype=jnp.int32)


def _boundary_rows(b, m):
    c = b.shape[0]
    if m >= SUBLANES // 2:
        g = c // (2 * m)
        b3 = b.reshape(g, 2 * m, LANES)
        return jnp.broadcast_to(b3[:, m - 1:m, :], (g, 2 * m, LANES)).reshape(c, LANES)
    if m == 2:
        b3 = b.reshape(c // SUBLANES, SUBLANES, LANES)
        lo = jnp.broadcast_to(b3[:, 1:2, :], b3.shape)
        hi = jnp.broadcast_to(b3[:, 5:6, :], b3.shape)
        sub = lax.broadcasted_iota(jnp.int32, b3.shape, 1)
        return jnp.where(sub < 4, lo, hi).reshape(c, LANES)
    prev = pltpu.roll(b, 1, axis=0)
    row = lax.broadcasted_iota(jnp.int32, b.shape, 0)
    return jnp.where((row & 1) == 1, prev, b)


def _hgrn_chunk(q, k, v, b, s0, lvl):
    c = q.shape[0]
    o = _dot((q * jnp.exp(b)).astype(BF16), s0.astype(BF16))
    a = jnp.where(lvl == -1, _dot_nt(q.astype(BF16), k.astype(BF16)), 0.0)
    m = c // 2
    level = int(np.log2(m))
    while m >= 1:
        e = jnp.exp(-jnp.abs(b - _boundary_rows(b, m)))
        a = jnp.where(lvl == level, _dot_nt((q * e).astype(BF16), (k * e).astype(BF16)), a)
        m //= 2
        level -= 1
    o = o + _dot(a.astype(BF16), v)
    bl = b[c - 1:c, :]
    kend = (k * jnp.exp(bl - b)).T.astype(BF16)
    decay = jnp.broadcast_to(jnp.exp(bl), (SUBLANES, LANES)).T[:, 0:1]
    return o, s0 * decay + _dot(kend, v)


def _hgrn_kernel(nchunks, q_ref, k_ref, v_ref, b_ref, lvl_ref, o_ref, s_ref, s_scr):
    t = pl.program_id(2)

    @pl.when(t == 0)
    def _():
        s_scr[...] = jnp.zeros_like(s_scr)

    lvl = lvl_ref[...]
    s = s_scr[...]
    for ci in range(nchunks):
        sl = slice(ci * CHUNK, (ci + 1) * CHUNK)
        o, s = _hgrn_chunk(q_ref[0, sl, :].astype(F32), k_ref[0, sl, :].astype(F32), v_ref[0, sl, :],
                           b_ref[0, sl, :], s, lvl)
        o_ref[0, sl, :] = o
    s_scr[...] = s

    @pl.when(t == pl.num_programs(2) - 1)
    def _():
        s_ref[0, 0] = s


def _hgrn_prompt(qb, kb, vb, bcum):
    bsz, lp, _ = qb.shape
    rows = HGRN_ROWS
    nt = lp // rows
    blk = pl.BlockSpec((1, rows, B_DIM), lambda b, h, t: (b, t, h))
    return pl.pallas_call(
        functools.partial(_hgrn_kernel, rows // CHUNK),
        grid=(bsz, B_HEADS, nt),
        in_specs=[blk, blk, blk, blk, pl.BlockSpec((CHUNK, CHUNK), lambda b, h, t: (0, 0))],
        out_specs=[blk, pl.BlockSpec((1, 1, B_DIM, B_DIM), lambda b, h, t: (b, h, 0, 0))],
        out_shape=[jax.ShapeDtypeStruct((bsz, lp, B_WIDTH), F32),
                   jax.ShapeDtypeStruct((bsz, B_HEADS, B_DIM, B_DIM), F32)],
        scratch_shapes=[pltpu.VMEM((B_DIM, B_DIM), F32)],
        compiler_params=_params(("arbitrary", "arbitrary", "arbitrary")),
        name="hgrn_prompt",
    )(qb, kb, vb, bcum, _level_table(CHUNK))


def _out_kernel(oa_ref, sza_ref, ob_ref, szb_ref, x_ref, w_ref, on_ref, pn_ref, y_ref):
    ya = (oa_ref[0] * sza_ref[0].astype(F32)).astype(BF16)
    y = _dot(ya, w_ref[0:A_WIDTH, :])
    ob = ob_ref[0]
    szb = szb_ref[0].astype(F32)
    for h in range(B_HEADS):
        sl = slice(h * B_DIM, (h + 1) * B_DIM)
        obh = ob[:, sl]
        ms = jnp.mean(obh * obh, axis=-1, keepdims=True)
        ybh = (obh * lax.rsqrt(ms + RMS_EPS) * on_ref[...]) * szb[:, sl]
        y = y + _dot(ybh.astype(BF16), w_ref[A_WIDTH + h * B_DIM:A_WIDTH + (h + 1) * B_DIM, :])
    ms = jnp.mean(y * y, axis=-1, keepdims=True)
    y_ref[0] = x_ref[0] + y * lax.rsqrt(ms + RMS_EPS) * pn_ref[...]


def _out_projection(oa, sza, ob, szb, x, wo, onorm, pnorm, skip_blocks):
    bsz, rows, d = x.shape
    tm = ROW_TILE
    nb = rows // tm
    in_map = lambda b, j: (b, j + skip_blocks, 0)
    const2 = lambda b, j: (0, 0)
    return pl.pallas_call(
        _out_kernel,
        grid=(bsz, nb),
        in_specs=[pl.BlockSpec((1, tm, A_WIDTH), in_map), pl.BlockSpec((1, tm, A_WIDTH), in_map),
                  pl.BlockSpec((1, tm, B_WIDTH), in_map), pl.BlockSpec((1, tm, B_WIDTH), in_map),
                  pl.BlockSpec((1, tm, d), lambda b, j: (b, j, 0)),
                  pl.BlockSpec(wo.shape, const2), pl.BlockSpec(onorm.shape, const2),
                  pl.BlockSpec(pnorm.shape, const2)],
        out_specs=pl.BlockSpec((1, tm, d), lambda b, j: (b, j, 0)),
        out_shape=jax.ShapeDtypeStruct((bsz, rows, d), F32),
        compiler_params=_params(("arbitrary", "arbitrary")),
        name="out_projection",
    )(oa, sza, ob, szb, x, wo, onorm, pnorm)


def _page_cumsum_kernel(x_ref, t_ref, o_ref):
    o_ref[...] = _tri_dot_rhs(x_ref[...], t_ref[...])


def _tri_dot_rhs(x, tri):
    hi, mid, lo = _split3(x)
    return _dot(hi, tri) + _dot(mid, tri) + _dot(lo, tri)


def _page_cumsum(logf_flat, page):
    npool, width = logf_flat.shape
    src = np.arange(width)
    dst = np.arange(width)
    tmat = ((src[:, None] % A_HEADS) == (dst[None, :] // page)) & ((src[:, None] // A_HEADS) <= (dst[None, :] % page))
    tmat = jnp.asarray(tmat, dtype=BF16)
    pb = PAGE_BLOCK
    return pl.pallas_call(
        _page_cumsum_kernel,
        grid=(npool // pb,),
        in_specs=[pl.BlockSpec((pb, width), lambda i: (i, 0)), pl.BlockSpec((width, width), lambda i: (0, 0))],
        out_specs=pl.BlockSpec((pb, width), lambda i: (i, 0)),
        out_shape=jax.ShapeDtypeStruct((npool, width), F32),
        compiler_params=_params(("arbitrary",)),
        name="page_cumsum",
    )(logf_flat, tmat)


def _fox_dec_kernel(ntok, pt_ref, q_ref, dq_ref, kn_ref, vn_ref, kc_ref, vc_ref, cp_ref, o_ref,
                    qblk_scr, m_scr, l_scr, acc_scr, carry_scr):
    pg = pl.program_id(1)
    npg = pl.num_programs(1)
    rows = ntok * A_HEADS
    head_of_row = lax.broadcasted_iota(jnp.int32, (A_HEADS, A_WIDTH), 0)
    head_of_lane = lax.broadcasted_iota(jnp.int32, (A_HEADS, A_WIDTH), 1) // A_HEAD_DIM
    own = head_of_row == head_of_lane

    @pl.when(pg == 0)
    def _():
        for t in range(ntok):
            qt = jnp.broadcast_to(q_ref[0, t:t + 1, :], (A_HEADS, A_WIDTH))
            qblk_scr[t * A_HEADS:(t + 1) * A_HEADS, :] = jnp.where(own, qt, 0.0)
        m_scr[...] = jnp.full(m_scr.shape, NEG, F32)
        l_scr[...] = jnp.zeros(l_scr.shape, F32)
        acc_scr[...] = jnp.zeros(acc_scr.shape, F32)
        carry_scr[...] = jnp.zeros(carry_scr.shape, F32)

    qblk = qblk_scr[...]
    kp = kc_ref[0].astype(BF16)
    vp = vc_ref[0].astype(BF16)
    cp = cp_ref[0]
    ck = carry_scr[...] + cp
    s = _dot_nt(qblk.astype(BF16), kp) - jnp.concatenate([ck] * ntok, axis=0)
    m_prev = m_scr[...]
    m_next = jnp.maximum(m_prev, jnp.max(s, axis=1, keepdims=True))
    alpha = jnp.exp(m_prev - m_next)
    p = jnp.exp(s - m_next)
    l_new = alpha * l_scr[...] + jnp.sum(p, axis=1, keepdims=True)
    acc_new = acc_scr[...] * alpha[:, 0:1] + _dot(p.astype(BF16), vp)
    carry_new = carry_scr[...] + cp[:, LANES - 1:LANES]
    m_scr[...] = m_next
    l_scr[...] = l_new
    acc_scr[...] = acc_new
    carry_scr[...] = carry_new

    @pl.when(pg == npg - 1)
    def _():
        m = m_next
        l = l_new
        acc = acc_new
        dqt = dq_ref[0].T[0:A_HEADS, :]
        tok_of_row = lax.broadcasted_iota(jnp.int32, (rows, 1), 0) // A_HEADS
        ctot = jnp.concatenate([carry_new] * ntok, axis=0)[:, 0:1]
        for t in range(ntok):
            kt = kn_ref[0, t:t + 1, :]
            vt = vn_ref[0, t:t + 1, :]
            st = jnp.sum(qblk * kt, axis=1, keepdims=True)
            ckt = ctot + jnp.concatenate([dqt[:, t:t + 1]] * ntok, axis=0)
            st = jnp.where(tok_of_row >= t, st - ckt, NEG)
            m2 = jnp.maximum(m, st)
            a2 = jnp.exp(m - m2)
            pt = jnp.exp(st - m2)
            l = a2 * l + pt
            acc = acc * a2[:, 0:1] + pt[:, 0:1] * vt
            m = m2
        o = acc / l[:, 0:1]
        for t in range(ntok):
            ot = jnp.where(own, o[t * A_HEADS:(t + 1) * A_HEADS, :], 0.0)
            o_ref[0, t:t + 1, :] = jnp.sum(ot, axis=0, keepdims=True)
        if ntok < DEC_TOK:
            o_ref[0, ntok:DEC_TOK, :] = jnp.zeros((DEC_TOK - ntok, A_WIDTH), F32)


def _fox_sample(page_table, q8, dq8, kn8, vn8, cache_k, cache_v, cpage, ntok):
    dbs, npages = page_table.shape
    _, page, width = cache_k.shape
    rows = ntok * A_HEADS
    tok_map = lambda b, g, pt: (b, 0, 0)
    page_map = lambda b, g, pt: (pt[b, g], 0, 0)
    grid_spec = pltpu.PrefetchScalarGridSpec(
        num_scalar_prefetch=1,
        grid=(dbs, npages),
        in_specs=[pl.BlockSpec((1, DEC_TOK, A_WIDTH), tok_map), pl.BlockSpec((1, DEC_TOK, LANES), tok_map),
                  pl.BlockSpec((1, DEC_TOK, A_WIDTH), tok_map), pl.BlockSpec((1, DEC_TOK, A_WIDTH), tok_map),
                  pl.BlockSpec((1, page, width), page_map), pl.BlockSpec((1, page, width), page_map),
                  pl.BlockSpec((1, A_HEADS, page), page_map)],
        out_specs=pl.BlockSpec((1, DEC_TOK, A_WIDTH), tok_map),
        scratch_shapes=[pltpu.VMEM((rows, A_WIDTH), F32), pltpu.VMEM((rows, LANES), F32),
                        pltpu.VMEM((rows, LANES), F32), pltpu.VMEM((rows, A_WIDTH), F32),
                        pltpu.VMEM((A_HEADS, LANES), F32)])
    return pl.pallas_call(
        functools.partial(_fox_dec_kernel, ntok),
        grid_spec=grid_spec,
        out_shape=jax.ShapeDtypeStruct((dbs, DEC_TOK, A_WIDTH), F32),
        compiler_params=_params(("arbitrary", "arbitrary")),
        name="fox_sample",
    )(page_table, q8, dq8, kn8, vn8, cache_k, cache_v, cpage)


def _hgrn_dec_kernel(ntok, nb, q_ref, k_ref, v_ref, b_ref, s_ref, o_ref, so_ref):
    tok = lax.broadcasted_iota(jnp.int32, (DEC_TOK, 1), 0)
    for i in range(nb):
        for h in range(B_HEADS):
            sl = slice(h * B_DIM, (h + 1) * B_DIM)
            q = q_ref[i, :, sl]
            k = k_ref[i, :, sl]
            v = v_ref[i, :, sl]
            b = b_ref[i, :, sl]
            s0 = s_ref[i, h]
            o = _dot((q * jnp.exp(b)).astype(BF16), s0.astype(BF16))
            bl = b[ntok - 1:ntok, :]
            kend = k * jnp.exp(bl - b)
            stack = jnp.where(tok < ntok, kend, jnp.where(tok == ntok, jnp.exp(bl), 0.0))
            cols = stack.T
            s_new = s0 * cols[:, ntok:ntok + 1]
            for s in range(ntok):
                ks, vs, bs = k[s:s + 1, :], v[s:s + 1, :], b[s:s + 1, :]
                later = tok >= s
                w = jnp.exp(jnp.where(later, b - bs, 0.0))
                a = jnp.sum(jnp.where(later, q * ks * w, 0.0), axis=1, keepdims=True)
                o = o + a * vs
                s_new = s_new + cols[:, s:s + 1] * vs
            o_ref[i, :, sl] = o
            so_ref[i, h] = s_new


def _hgrn_sample(q8, k8, v8, b8, state, ntok):
    dbs = q8.shape[0]
    nb = HGRN_DEC_BATCH
    tokspec = pl.BlockSpec((nb, DEC_TOK, B_WIDTH), lambda i: (i, 0, 0))
    stspec = pl.BlockSpec((nb, B_HEADS, B_DIM, B_DIM), lambda i: (i, 0, 0, 0))
    return pl.pallas_call(
        functools.partial(_hgrn_dec_kernel, ntok, nb),
        grid=(dbs // nb,),
        in_specs=[tokspec, tokspec, tokspec, tokspec, stspec],
        out_specs=[tokspec, stspec],
        out_shape=[jax.ShapeDtypeStruct((dbs, DEC_TOK, B_WIDTH), F32),
                   jax.ShapeDtypeStruct(state.shape, F32)],
        compiler_params=_params(("arbitrary",)),
        name="hgrn_sample",
    )(q8, k8, v8, b8, state)


def _pad_tokens(x, ntok, fill_last=False):
    n = x.shape[-1]
    x = x.astype(F32).reshape(-1, ntok, n)
    extra = DEC_TOK - ntok
    if fill_last:
        tail = jnp.broadcast_to(x[:, ntok - 1:ntok, :], (x.shape[0], extra, n))
    else:
        tail = jnp.zeros((x.shape[0], extra, n), F32)
    return jnp.concatenate([x, tail], axis=1)


def kernel(x_prompt, x_sample, cache_k, cache_v, cache_logf, state_hgrn, page_table, meta_tokens, w_in, b_forget,
           hgrn_lower_bound, hgrn_out_norm, pre_norm, post_norm, w_out):
    bp, seq, d = x_prompt.shape
    dbs, ntok, _ = x_sample.shape
    depth, npool, page = cache_k.shape[:3]
    assert depth == 1 and w_in.shape[0] == 1, "single-layer trunk"
    assert seq % ROW_TILE == 0 and (seq + ROW_TILE) % FOX_TQ == 0
    assert (seq + ROW_TILE) % HGRN_ROWS == 0 and (dbs * ntok) % ROW_TILE == 0
    assert npool % PAGE_BLOCK == 0 and dbs % HGRN_DEC_BATCH == 0 and page == LANES and ntok < DEC_TOK

    w = w_in[0]
    offs = np.cumsum((A_WIDTH, A_WIDTH, A_WIDTH, A_HEADS, A_WIDTH, B_WIDTH, B_WIDTH, B_WIDTH, B_WIDTH))
    w_qa, w_ka, w_va, w_fa, w_za, w_qb, w_fb, w_vb, w_zb = jnp.split(w, [int(o) for o in offs[:-1]], axis=1)
    fa_rep = jnp.concatenate([w_fa] * BIAS_GROUPS + [jnp.zeros((d, LANES - BIAS_GROUPS * A_HEADS), w.dtype)], axis=1)
    wp = jnp.concatenate([w_qa, w_ka, w_va, w_za, w_qb, w_fb, w_vb, w_zb, fa_rep], axis=1).astype(BF16)
    bf_rep = jnp.concatenate([b_forget[0]] * BIAS_GROUPS + [jnp.zeros((LANES - BIAS_GROUPS * A_HEADS,), F32)])[None, :]
    g_pre = pre_norm[0][None, :]
    wo = w_out[0].astype(BF16)
    onorm = hgrn_out_norm[0][None, :]
    pnorm = post_norm[0][None, :]
    pad = ROW_TILE - N_META
    meta_blk = jnp.concatenate([jnp.zeros((pad, d), F32), meta_tokens.astype(F32)], axis=0)

    (q, k32, v32, kbf, vbf, qbias, kbias, lfa, _, sza, qb, kb, vb, bcum, szb) = _in_projection(
        x_prompt, meta_blk, wp, g_pre, bf_rep, hgrn_lower_bound, True, ROW_TILE, CHUNK)
    oa = _fox_prompt(q, qbias, kbf, kbias, vbf)
    ob, s_prompt = _hgrn_prompt(qb, kb, vb, bcum)
    y_prompt = _out_projection(oa, sza, ob, szb, x_prompt, wo, onorm, pnorm, 1)
    lp = q.shape[1]
    prompt_k = k32[:, pad:, :].reshape(1, bp, lp - pad, A_HEADS, A_HEAD_DIM)
    prompt_v = v32[:, pad:, :].reshape(1, bp, lp - pad, A_HEADS, A_HEAD_DIM)
    prompt_logf = lfa[:, pad:, :][None]

    xs = x_sample.reshape(1, dbs * ntok, d)
    (qs, k32s, v32s, _, _, _, _, lfas, cas, szas, qbs, kbs, vbs, bcums, szbs) = _in_projection(
        xs, meta_blk, wp, g_pre, bf_rep, hgrn_lower_bound, False, ntok, ntok)
    cpage = _page_cumsum(cache_logf[0].reshape(npool, page * A_HEADS), page).reshape(npool, A_HEADS, page)
    oas = _fox_sample(page_table, _pad_tokens(qs[0], ntok), _pad_tokens(cas[0], ntok),
                      _pad_tokens(k32s[0].astype(BF16), ntok), _pad_tokens(v32s[0].astype(BF16), ntok),
                      cache_k[0].reshape(npool, page, A_WIDTH), cache_v[0].reshape(npool, page, A_WIDTH),
                      cpage, ntok)
    obs, s_sample = _hgrn_sample(_pad_tokens(qbs[0], ntok), _pad_tokens(kbs[0], ntok), _pad_tokens(vbs[0], ntok),
                                 _pad_tokens(bcums[0], ntok, fill_last=True), state_hgrn[0], ntok)
    oas = oas[:, :ntok, :].reshape(1, dbs * ntok, A_WIDTH)
    obs = obs[:, :ntok, :].reshape(1, dbs * ntok, B_WIDTH)
    y_sample = _out_projection(oas, szas, obs, szbs, xs, wo, onorm, pnorm, 0).reshape(dbs, ntok, d)

    return (y_prompt, y_sample, prompt_k, prompt_v, prompt_logf, s_prompt[None],
            k32s.reshape(1, dbs, ntok, A_HEADS, A_HEAD_DIM), v32s.reshape(1, dbs, ntok, A_HEADS, A_HEAD_DIM),
            lfas.reshape(1, dbs, ntok, A_HEADS), s_sample[None])
```
